```python
import math
import jax, jax.numpy as jnp
from jax import lax
import numpy as np

D_MODEL = 1024
BATCH = 8
SEQ = 2048
DEPTH = 1

PLE_DIM = 256
SSD_HEADS = 16
SSD_HEAD_DIM = 64
SSD_INNER = SSD_HEADS * SSD_HEAD_DIM
SSD_GROUPS = 2
SSD_STATE = 128
SSD_CONV = 4
SSD_CHUNK = 128
MLA_HEADS = 16
MLA_Q_RANK = 384
MLA_KV_RANK = 256
MLA_NOPE = 64
MLA_ROPE = 32
MLA_V = 64
MLA_OUT = MLA_HEADS * MLA_V
ROPE_BASE = 10000.0
Q_BLOCK = 128
MIX_WIDTH = SSD_INNER + MLA_OUT
SSD_XBC = SSD_INNER + 2 * SSD_GROUPS * SSD_STATE
IN_WIDTH = SSD_INNER + SSD_XBC + SSD_HEADS + MLA_Q_RANK + MLA_KV_RANK + MLA_ROPE
D_FF = -(-8 * D_MODEL // (3 * 256)) * 256
DEEPNORM_ALPHA = (2 * DEPTH) ** 0.25
DEEPNORM_BETA = (8 * DEPTH) ** -0.25
EPS = 1e-6

kernel_name = "hybrid_ssd_mla_deepnorm_ple_layer"


def rmsnorm(u, g):
    uf = u.astype(jnp.float32)
    out = uf * lax.rsqrt(jnp.mean(uf * uf, axis=-1, keepdims=True) + EPS)
    return (out * g.astype(jnp.float32)).astype(u.dtype)


def layernorm(u, g, b):
    uf = u.astype(jnp.float32)
    mu = jnp.mean(uf, axis=-1, keepdims=True)
    d = uf - mu
    var = jnp.mean(d * d, axis=-1, keepdims=True)
    out = d * lax.rsqrt(var + 1e-5) * g.astype(jnp.float32) + b.astype(jnp.float32)
    return out.astype(u.dtype)


def rope_tables(positions):
    inv_freq = 1.0 / (ROPE_BASE ** (jnp.arange(0, MLA_ROPE, 2, dtype=jnp.float32) / MLA_ROPE))
    ang = positions.astype(jnp.float32)[..., None] * inv_freq
    return jnp.cos(ang), jnp.sin(ang)


def apply_rope(u, cos, sin):
    cos = cos.astype(u.dtype)
    sin = sin.astype(u.dtype)
    u1, u2 = jnp.split(u, 2, axis=-1)
    return jnp.concatenate([u1 * cos - u2 * sin, u2 * cos + u1 * sin], axis=-1)


def causal_depthwise_conv(u, w, bias):
    c = u.shape[-1]
    out = lax.conv_general_dilated(
        u, w[:, None, :].astype(u.dtype), window_strides=(1,),
        padding=((SSD_CONV - 1, 0),), dimension_numbers=("NWC", "WIO", "NWC"),
        feature_group_count=c)
    return out + bias.astype(u.dtype)


def ssd_mixer(z, xBC, dt_raw, conv_w, conv_b, dt_bias, A_log, D_skip, norm_w):
    b, s, _ = xBC.shape
    G, E, P, N, L = SSD_GROUPS, SSD_HEADS // SSD_GROUPS, SSD_HEAD_DIM, SSD_STATE, SSD_CHUNK
    nc = s // L
    f32 = jnp.float32
    xBC = jax.nn.silu(causal_depthwise_conv(xBC, conv_w, conv_b))
    xs, Bm, Cm = jnp.split(xBC, [SSD_INNER, SSD_INNER + G * N], axis=-1)
    xs = xs.astype(f32).reshape(b, nc, L, G, E, P)
    Bm = Bm.astype(f32).reshape(b, nc, L, G, N)
    Cm = Cm.astype(f32).reshape(b, nc, L, G, N)
    dt = jax.nn.softplus(dt_raw.astype(f32) + dt_bias.astype(f32))
    A = -jnp.exp(A_log.astype(f32))
    dt_c = dt.reshape(b, nc, L, G, E)
    dA_cs = jnp.cumsum((dt * A).reshape(b, nc, L, G, E).transpose(0, 1, 3, 4, 2), axis=-1)
    X = xs * dt_c[..., None]
    causal = jnp.tril(jnp.ones((L, L), dtype=bool))
    seg = dA_cs[..., :, None] - dA_cs[..., None, :]
    Lmat = jnp.exp(jnp.where(causal, seg, -jnp.inf))
    CB = jnp.einsum("bclgn,bcsgn->bcgls", Cm, Bm)
    y_diag = jnp.einsum("bcgls,bcgels,bcsgep->bclgep", CB, Lmat, X)
    decay_states = jnp.exp(dA_cs[..., -1:] - dA_cs)
    states = jnp.einsum("bclgn,bcgel,bclgep->bcgepn", Bm, decay_states, X)
    chunk_decay = jnp.exp(dA_cs[..., -1])

    def step(carry, inp):
        dec, st = inp
        return carry * dec[..., None, None] + st, carry

    init = jnp.zeros((b, G, E, P, N), f32)
    _, prev = lax.scan(step, init, (chunk_decay.transpose(1, 0, 2, 3), states.transpose(1, 0, 2, 3, 4, 5)))
    prev = prev.transpose(1, 0, 2, 3, 4, 5)
    y_off = jnp.einsum("bclgn,bcgepn,bcgel->bclgep", Cm, prev, jnp.exp(dA_cs))
    y = y_diag + y_off + D_skip.astype(f32).reshape(G, E)[:, :, None] * xs
    y = y.reshape(b, s, SSD_INNER)
    y = rmsnorm(y * jax.nn.silu(z.astype(f32)), norm_w)
    return y.astype(z.dtype)


def mla_mixer(q_c, kv_c, k_rope, cos, sin, q_norm_w, w_q_b, kv_norm_w, w_kv_b, out_norm_w):
    b, s, _ = q_c.shape
    H = MLA_HEADS
    q = (rmsnorm(q_c, q_norm_w) @ w_q_b).reshape(b, s, H, MLA_NOPE + MLA_ROPE)
    q_nope, q_rope = jnp.split(q, [MLA_NOPE], axis=-1)
    kv = (rmsnorm(kv_c, kv_norm_w) @ w_kv_b).reshape(b, s, H, MLA_NOPE + MLA_V)
    k_nope, v = jnp.split(kv, [MLA_NOPE], axis=-1)
    q_rope = apply_rope(q_rope, cos[:, :, None, :], sin[:, :, None, :])
    k_rope = apply_rope(k_rope, cos, sin)
    scale = 1.0 / math.sqrt(MLA_NOPE + MLA_ROPE)
    nb = s // Q_BLOCK
    qn_blocks = q_nope.reshape(b, nb, Q_BLOCK, H, MLA_NOPE).transpose(1, 0, 2, 3, 4)
    qr_blocks = q_rope.reshape(b, nb, Q_BLOCK, H, MLA_ROPE).transpose(1, 0, 2, 3, 4)
    key_idx = jnp.arange(s)

    def attend(args):
        qn, qr, blk = args
        sc = (jnp.einsum("bqhd,bkhd->bhqk", qn, k_nope).astype(jnp.float32)
              + jnp.einsum("bqhr,bkr->bhqk", qr, k_rope).astype(jnp.float32)) * scale
        q_idx = blk * Q_BLOCK + jnp.arange(Q_BLOCK)
        sc = jnp.where(q_idx[:, None] >= key_idx[None, :], sc, -jnp.inf)
        pr = jax.nn.softmax(sc, axis=-1).astype(v.dtype)
        return jnp.einsum("bhqk,bkhd->bqhd", pr, v)

    out = lax.map(attend, (qn_blocks, qr_blocks, jnp.arange(nb)))
    out = out.transpose(1, 0, 2, 3, 4).reshape(b, s, MLA_OUT)
    return rmsnorm(out, out_norm_w)


def setup_inputs(seed: int = 0) -> dict:
    key = jax.random.key(seed)
    ks = iter(jax.random.split(key, 40))
    f32 = jnp.float32

    def w(shape, fan_in, scale=1.0):
        return jax.random.normal(next(ks), shape, f32) * (fan_in ** -0.5) * scale

    def gain(shape):
        return 1.0 + 0.02 * jax.random.normal(next(ks), shape, f32)

    def small(shape):
        return 0.02 * jax.random.normal(next(ks), shape, f32)

    x = jax.random.normal(next(ks), (BATCH, SEQ, D_MODEL), f32)
    p = jax.random.normal(next(ks), (DEPTH, BATCH, SEQ, PLE_DIM), f32)
    offsets = jax.random.randint(next(ks), (BATCH, 1), 0, 1024, dtype=jnp.int32)
    positions = (jnp.arange(SEQ, dtype=jnp.int32)[None, :] + offsets).astype(jnp.int32)

    dt0 = jnp.exp(jax.random.uniform(next(ks), (DEPTH, SSD_HEADS), f32) * (math.log(0.1) - math.log(0.001)) + math.log(0.001))
    ssd_dt_bias = dt0 + jnp.log(-jnp.expm1(-dt0))
    ssd_A_log = jnp.log(jax.random.uniform(next(ks), (DEPTH, SSD_HEADS), f32, 1.0, 16.0))

    return {
        "x": x,
        "p": p,
        "positions": positions,
        "w_in": w((DEPTH, D_MODEL, IN_WIDTH), D_MODEL),
        "ssd_conv_w": w((DEPTH, SSD_CONV, SSD_XBC), SSD_CONV),
        "ssd_conv_b": small((DEPTH, SSD_XBC)),
        "ssd_dt_bias": ssd_dt_bias,
        "ssd_A_log": ssd_A_log,
        "ssd_D": gain((DEPTH, SSD_HEADS)),
        "ssd_norm_w": gain((DEPTH, SSD_INNER)),
        "mla_q_norm_w": gain((DEPTH, MLA_Q_RANK)),
        "mla_w_q_b": w((DEPTH, MLA_Q_RANK, MLA_HEADS * (MLA_NOPE + MLA_ROPE)), MLA_Q_RANK),
        "mla_kv_norm_w": gain((DEPTH, MLA_KV_RANK)),
        "mla_w_kv_b": w((DEPTH, MLA_KV_RANK, MLA_HEADS * (MLA_NOPE + MLA_V)), MLA_KV_RANK),
        "mla_out_norm_w": gain((DEPTH, MLA_OUT)),
        "w_out": w((DEPTH, MIX_WIDTH, D_MODEL), MIX_WIDTH, DEEPNORM_BETA),
        "ln_mix_g": gain((DEPTH, D_MODEL)),
        "ln_mix_b": small((DEPTH, D_MODEL)),
        "w_ffn_gate": w((DEPTH, D_MODEL, D_FF), D_MODEL),
        "w_ffn_up": w((DEPTH, D_MODEL, D_FF), D_MODEL),
        "w_ffn_down": w((DEPTH, D_FF, D_MODEL), D_FF, DEEPNORM_BETA),
        "w_ple_gate": w((DEPTH, D_MODEL, D_MODEL), D_MODEL),
        "w_ple_proj": w((DEPTH, PLE_DIM, D_MODEL), PLE_DIM, DEEPNORM_BETA),
        "ln_ffn_g": gain((DEPTH, D_MODEL)),
        "ln_ffn_b": small((DEPTH, D_MODEL)),
    }


def reference(x, p, positions, w_in, ssd_conv_w, ssd_conv_b, ssd_dt_bias, ssd_A_log, ssd_D, ssd_norm_w,
              mla_q_norm_w, mla_w_q_b, mla_kv_norm_w, mla_w_kv_b, mla_out_norm_w, w_out,
              ln_mix_g, ln_mix_b, w_ffn_gate, w_ffn_up, w_ffn_down, w_ple_gate, w_ple_proj,
              ln_ffn_g, ln_ffn_b):
    s0 = SSD_INNER
    s1 = s0 + SSD_XBC
    s2 = s1 + SSD_HEADS
    s3 = s2 + MLA_Q_RANK
    s4 = s3 + MLA_KV_RANK
    splits = [s0, s1, s2, s3, s4]
    cos, sin = rope_tables(positions)
    h = x
    for i in range(DEPTH):
        proj = h @ w_in[i]
        z, xBC, dt_raw, q_c, kv_c, k_rope = jnp.split(proj, splits, axis=-1)
        y_ssd = ssd_mixer(z, xBC, dt_raw, ssd_conv_w[i], ssd_conv_b[i], ssd_dt_bias[i],
                          ssd_A_log[i], ssd_D[i], ssd_norm_w[i])
        y_mla = mla_mixer(q_c, kv_c, k_rope, cos, sin, mla_q_norm_w[i], mla_w_q_b[i],
                          mla_kv_norm_w[i], mla_w_kv_b[i], mla_out_norm_w[i])
        mix = jnp.concatenate([y_ssd, y_mla], axis=-1) @ w_out[i]
        h = layernorm(DEEPNORM_ALPHA * h + mix, ln_mix_g[i], ln_mix_b[i])
        ffn = (jax.nn.silu(h @ w_ffn_gate[i]) * (h @ w_ffn_up[i])) @ w_ffn_down[i]
        ple = jax.nn.sigmoid(h @ w_ple_gate[i]) * (p[i] @ w_ple_proj[i])
        h = layernorm(DEEPNORM_ALPHA * h + ffn + ple, ln_ffn_g[i], ln_ffn_b[i])
    return h
```

```python
import functools
import math

import jax
import jax.numpy as jnp
from jax import lax
from jax.experimental import pallas as pl
from jax.experimental.pallas import tpu as pltpu

D_MODEL = 1024
PLE_DIM = 256
SSD_HEADS = 16
SSD_HEAD_DIM = 64
SSD_INNER = SSD_HEADS * SSD_HEAD_DIM
SSD_GROUPS = 2
SSD_STATE = 128
SSD_CONV = 4
SSD_CHUNK = 128
SSD_XBC = SSD_INNER + 2 * SSD_GROUPS * SSD_STATE
MLA_HEADS = 16
MLA_Q_RANK = 384
MLA_KV_RANK = 256
MLA_NOPE = 64
MLA_ROPE = 32
MLA_V = 64
MLA_OUT = MLA_HEADS * MLA_V
ROPE_BASE = 10000.0
D_FF = 2816
DEPTH = 1
DEEPNORM_ALPHA = (2 * DEPTH) ** 0.25
EPS = 1e-6
LN_EPS = 1e-5

LANES = 128
HEAD_PAD = 128
SMALL_W = 128
KR_OFF = 64
VMEM_LIMIT = 56 * 1024 * 1024

F32 = jnp.float32
BF16 = jnp.bfloat16


def _dot(a, b):
    return jnp.dot(a, b, preferred_element_type=F32)


def _dot_nt(a, b):
    return lax.dot_general(a, b, (((1,), (1,)), ((), ())), preferred_element_type=F32)


def _rmsnorm(u, g):
    ms = jnp.mean(u * u, axis=-1, keepdims=True)
    return u * lax.rsqrt(ms + EPS) * g


def _layernorm(u, g, b):
    mu = jnp.mean(u, axis=-1, keepdims=True)
    d = u - mu
    var = jnp.mean(d * d, axis=-1, keepdims=True)
    return d * lax.rsqrt(var + LN_EPS) * g + b


def _rope_chunk(c, cm, s1, s2):
    return c * cm + pltpu.roll(c, 16, 1) * s1 + pltpu.roll(c, LANES - 16, 1) * s2


def _const_spec(shape):
    nd = len(shape)
    return pl.BlockSpec(shape, lambda *_: (0,) * nd, pipeline_mode=pl.Buffered(1))


_C_Z = (0, SSD_INNER)
_C_XBC = (_C_Z[1], _C_Z[1] + SSD_XBC)
_C_QC = (_C_XBC[1], _C_XBC[1] + MLA_Q_RANK)
_C_KVC = (_C_QC[1], _C_QC[1] + MLA_KV_RANK)
_C_SMALL = (_C_KVC[1], _C_KVC[1] + SMALL_W)
IN_CAT = _C_SMALL[1]


def _inproj_kernel(x_ref, win_ref, wq_ref, wk_ref, wv_ref, qg_ref, kvg_ref, cm_ref, s1_ref, s2_ref,
                   z_ref, xbc_ref, small_ref, q_ref, k_ref, v_ref):
    xb = x_ref[...].astype(BF16)
    z_ref[...] = _dot(xb, win_ref[:, _C_Z[0]:_C_Z[1]]).astype(BF16)
    xbc_ref[...] = _dot(xb, win_ref[:, _C_XBC[0]:_C_XBC[1]]).astype(BF16)
    qc = _dot(xb, win_ref[:, _C_QC[0]:_C_QC[1]])
    kvc = _dot(xb, win_ref[:, _C_KVC[0]:_C_KVC[1]])
    small = _dot(xb, win_ref[:, _C_SMALL[0]:_C_SMALL[1]])
    small_ref[...] = small

    cm = cm_ref[...]
    s1 = s1_ref[...]
    s2 = s2_ref[...]
    scale = 1.0 / math.sqrt(MLA_NOPE + MLA_ROPE)

    qn = _rmsnorm(qc, qg_ref[...]).astype(BF16)
    q = _dot(qn, wq_ref[...])
    for h in range(MLA_HEADS):
        sl = slice(h * HEAD_PAD, (h + 1) * HEAD_PAD)
        q_ref[:, sl] = (_rope_chunk(q[:, sl], cm, s1, s2) * scale).astype(BF16)

    kvn = _rmsnorm(kvc, kvg_ref[...]).astype(BF16)
    v_ref[...] = _dot(kvn, wv_ref[...]).astype(BF16)
    kn = _dot(kvn, wk_ref[...])
    lane = lax.broadcasted_iota(jnp.int32, small.shape, 1)
    kr = jnp.where(lane >= KR_OFF, _rope_chunk(small, cm, s1, s2), 0.0)
    for h in range(MLA_HEADS):
        sl = slice(h * HEAD_PAD, (h + 1) * HEAD_PAD)
        k_ref[:, sl] = (kn[:, sl] + kr).astype(BF16)


def _inproj(x2, w_cat, wq, wk, wv, qg, kvg, cm, s1, s2, tm):
    n = x2.shape[0]
    row = lambda w: pl.BlockSpec((tm, w), lambda i: (i, 0))
    qkw = MLA_HEADS * HEAD_PAD
    out_shape = (
        jax.ShapeDtypeStruct((n, SSD_INNER), BF16),
        jax.ShapeDtypeStruct((n, SSD_XBC), BF16),
        jax.ShapeDtypeStruct((n, SMALL_W), F32),
        jax.ShapeDtypeStruct((n, qkw), BF16),
        jax.ShapeDtypeStruct((n, qkw), BF16),
        jax.ShapeDtypeStruct((n, MLA_OUT), BF16),
    )
    return pl.pallas_call(
        _inproj_kernel,
        grid=(n // tm,),
        in_specs=[row(D_MODEL), _const_spec(w_cat.shape), _const_spec(wq.shape), _const_spec(wk.shape),
                  _const_spec(wv.shape), _const_spec(qg.shape), _const_spec(kvg.shape),
                  row(LANES), row(LANES), row(LANES)],
        out_specs=(row(SSD_INNER), row(SSD_XBC), row(SMALL_W), row(qkw), row(qkw), row(MLA_OUT)),
        out_shape=out_shape,
        compiler_params=pltpu.CompilerParams(dimension_semantics=("arbitrary",), vmem_limit_bytes=VMEM_LIMIT),
        name="inproj",
    )(x2, w_cat, wq, wk, wv, qg, kvg, cm, s1, s2)


_TAIL = 8
_GW = SSD_INNER // SSD_GROUPS


def _expand_heads(u, e_ref):
    hi = u.astype(BF16)
    lo = (u - hi.astype(F32)).astype(BF16)
    e = e_ref[...]
    return _dot(hi, e) + _dot(lo, e)


def _ssd_kernel(xbc_ref, z_ref, small_ref, dtt_ref, cw_ref, cb_ref, dtb_r_ref, dtb_c_ref, al_r_ref, al_c_ref,
                dfull_ref, nw_ref, e_ref, y_ref, state_ref, win_ref):
    L = SSD_CHUNK

    @pl.when(pl.program_id(1) == 0)
    def _():
        state_ref[...] = jnp.zeros_like(state_ref)
        win_ref[0:_TAIL, :] = jnp.zeros((_TAIL, SSD_XBC), F32)

    win_ref[_TAIL:_TAIL + L, :] = xbc_ref[...].astype(F32)
    acc = cb_ref[...] + cw_ref[SSD_CONV - 1:SSD_CONV, :] * win_ref[_TAIL:_TAIL + L, :]
    for k in range(SSD_CONV - 1):
        off = _TAIL - (SSD_CONV - 1) + k
        acc = acc + cw_ref[k:k + 1, :] * win_ref[off:off + L, :]
    win_ref[0:_TAIL, :] = win_ref[L:L + _TAIL, :]
    xbc = acc * jax.nn.sigmoid(acc)
    xs = xbc[:, :SSD_INNER]
    bm = xbc[:, SSD_INNER:SSD_INNER + SSD_GROUPS * SSD_STATE]
    cmat = xbc[:, SSD_INNER + SSD_GROUPS * SSD_STATE:]

    ri = lax.broadcasted_iota(jnp.int32, (L, L), 0)
    ci = lax.broadcasted_iota(jnp.int32, (L, L), 1)
    causal = ri >= ci
    tril = causal.astype(F32)
    triu = (ri <= ci).astype(F32)
    dt = jax.nn.softplus(small_ref[:, 0:SSD_HEADS] + dtb_r_ref[...])
    da = dt * (-jnp.exp(al_r_ref[...]))
    cs = jnp.dot(tril, da, precision=lax.Precision.HIGHEST, preferred_element_type=F32)
    dtt = jax.nn.softplus(dtt_ref[...] + dtb_c_ref[...])
    dat = dtt * (-jnp.exp(al_c_ref[...]))
    cst = jnp.dot(dat, triu, precision=lax.Precision.HIGHEST, preferred_element_type=F32)

    u1 = jnp.exp(cs)
    u2 = dt * jnp.exp(cs[L - 1:L, :] - cs)
    u1f = _expand_heads(u1, e_ref)
    u2f = _expand_heads(u2, e_ref)
    chunk_decay = u1f[L - 1:L, :]

    lane = lax.broadcasted_iota(jnp.int32, (L, LANES), 1)
    low = lane < SSD_HEAD_DIM
    xs_b = xs.astype(BF16)
    xd_b = (xs * u2f).astype(BF16)

    y_parts = []
    for g in range(SSD_GROUPS):
        cg = cmat[:, g * SSD_STATE:(g + 1) * SSD_STATE].astype(BF16)
        bg_f = bm[:, g * SSD_STATE:(g + 1) * SSD_STATE]
        bg = bg_f.astype(BF16)
        cb = _dot_nt(cg, bg)
        gsl = slice(g * _GW, (g + 1) * _GW)
        st = state_ref[:, gsl]
        y_off = _dot(cg, st.astype(BF16)) * u1f[:, gsl]
        heads_per_group = SSD_HEADS // SSD_GROUPS
        for j in range(heads_per_group // 2):
            h0 = g * heads_per_group + 2 * j
            ms = []
            for h in (h0, h0 + 1):
                seg = cs[:, h:h + 1] - cst[h:h + 1, :]
                lm = jnp.where(causal, jnp.exp(seg), 0.0)
                ms.append((cb * lm * dtt[h:h + 1, :]).astype(BF16))
            lhs = jnp.concatenate(ms, axis=1)
            xp = xs_b[:, h0 * SSD_HEAD_DIM:h0 * SSD_HEAD_DIM + LANES]
            zero = jnp.zeros_like(xp)
            rhs = jnp.concatenate([jnp.where(low, xp, zero), jnp.where(low, zero, xp)], axis=0)
            y_parts.append(_dot(lhs, rhs) + y_off[:, 2 * j * SSD_HEAD_DIM:2 * j * SSD_HEAD_DIM + LANES])
        new_st = _dot(bg_f.T.astype(BF16), xd_b[:, gsl])
        state_ref[:, gsl] = st * chunk_decay[:, gsl] + new_st

    y = jnp.concatenate(y_parts, axis=1) + dfull_ref[...] * xs
    zf = z_ref[...].astype(F32)
    gated = y * (zf * jax.nn.sigmoid(zf))
    y_ref[...] = _rmsnorm(gated, nw_ref[...]).astype(BF16)


def _ssd(xbc, z, small, dtt, cw, cb, dtb, alog, dfull, nw, e, batch, nchunk):
    L = SSD_CHUNK
    n = xbc.shape[0]
    row = lambda w: pl.BlockSpec((L, w), lambda b, c: (b * nchunk + c, 0))
    h = SSD_HEADS
    return pl.pallas_call(
        _ssd_kernel,
        grid=(batch, nchunk),
        in_specs=[row(SSD_XBC), row(SSD_INNER), row(SMALL_W),
                  pl.BlockSpec((h, L), lambda b, c: (0, b * nchunk + c)),
                  _const_spec(cw.shape), _const_spec(cb.shape),
                  _const_spec((1, h)), _const_spec((h, 1)), _const_spec((1, h)), _const_spec((h, 1)),
                  _const_spec(dfull.shape), _const_spec(nw.shape), _const_spec(e.shape)],
        out_specs=row(SSD_INNER),
        out_shape=jax.ShapeDtypeStruct((n, SSD_INNER), BF16),
        scratch_shapes=[pltpu.VMEM((SSD_STATE, SSD_INNER), F32), pltpu.VMEM((L + _TAIL, SSD_XBC), F32)],
        compiler_params=pltpu.CompilerParams(dimension_semantics=("arbitrary", "arbitrary"),
                                             vmem_limit_bytes=VMEM_LIMIT),
        name="ssd",
    )(xbc, z, small, dtt, cw, cb, dtb.reshape(1, h), dtb.reshape(h, 1), alog.reshape(1, h), alog.reshape(h, 1),
      dfull, nw, e)


_NEG = -1e30


def _attn_kernel(q_ref, k_ref, v_ref, o_ref, m_ref, l_ref, acc_ref, *, seq, blk):
    nblk = seq // blk
    lane = lax.broadcasted_iota(jnp.int32, (blk, LANES), 1)
    low = lane < MLA_V
    ri = lax.broadcasted_iota(jnp.int32, (blk, blk), 0)
    ci = lax.broadcasted_iota(jnp.int32, (blk, blk), 1)
    diag_ok = ri >= ci

    def kv_step(qi, kj, masked):
        q0 = qi * blk
        k0 = pl.multiple_of(kj * blk, blk)
        ps = []
        alphas = []
        for h in range(2):
            hs = slice(h * HEAD_PAD, (h + 1) * HEAD_PAD)
            s = _dot_nt(q_ref[q0:q0 + blk, hs], k_ref[pl.ds(k0, blk), hs])
            if masked:
                s = jnp.where(diag_ok, s, _NEG)
            m_prev = m_ref[h]
            m_cur = jnp.maximum(m_prev, jnp.max(s, axis=1, keepdims=True))
            p = jnp.exp(s - m_cur[:, 0:1])
            alpha = jnp.exp(m_prev - m_cur)
            l_ref[h] = alpha * l_ref[h] + jnp.sum(p, axis=1, keepdims=True)
            m_ref[h] = m_cur
            ps.append(p.astype(BF16))
            alphas.append(alpha)
        vp = v_ref[pl.ds(k0, blk), :]
        zero = jnp.zeros_like(vp)
        vbd = jnp.concatenate([jnp.where(low, vp, zero), jnp.where(low, zero, vp)], axis=0)
        pv = _dot(jnp.concatenate(ps, axis=1), vbd)
        acc_ref[...] = acc_ref[...] * jnp.where(low, alphas[0], alphas[1]) + pv

    for qi in range(nblk):
        m_ref[...] = jnp.full(m_ref.shape, _NEG, F32)
        l_ref[...] = jnp.zeros(l_ref.shape, F32)
        acc_ref[...] = jnp.zeros(acc_ref.shape, F32)
        if qi > 0:
            def body(kj, carry, qi=qi):
                kv_step(qi, kj, False)
                return carry
            lax.fori_loop(0, qi, body, 0)
        kv_step(qi, qi, True)
        linv = jnp.where(low, 1.0 / l_ref[0], 1.0 / l_ref[1])
        o_ref[qi * blk:(qi + 1) * blk, :] = (acc_ref[...] * linv).astype(BF16)


def _attention(q, k, v, batch, seq, blk):
    n = q.shape[0]
    hp = MLA_HEADS // 2
    return pl.pallas_call(
        functools.partial(_attn_kernel, seq=seq, blk=blk),
        grid=(batch, hp),
        in_specs=[pl.BlockSpec((seq, 2 * HEAD_PAD), lambda b, h: (b, h)),
                  pl.BlockSpec((seq, 2 * HEAD_PAD), lambda b, h: (b, h)),
                  pl.BlockSpec((seq, 2 * MLA_V), lambda b, h: (b, h))],
        out_specs=pl.BlockSpec((seq, 2 * MLA_V), lambda b, h: (b, h)),
        out_shape=jax.ShapeDtypeStruct((n, MLA_OUT), BF16),
        scratch_shapes=[pltpu.VMEM((2, blk, LANES), F32), pltpu.VMEM((2, blk, LANES), F32),
                        pltpu.VMEM((blk, LANES), F32)],
        compiler_params=pltpu.CompilerParams(dimension_semantics=("arbitrary", "arbitrary"),
                                             vmem_limit_bytes=VMEM_LIMIT),
        name="attention",
    )(q, k, v)


_FF_SPLIT = (0, 1536, D_FF)


def _tail_kernel(ys_ref, at_ref, x_ref, p_ref, wo1_ref, wo2_ref, ong_ref, g1_ref, b1_ref,
                 wg_ref, wu_ref, wd_ref, wpg_ref, wpp_ref, g2_ref, b2_ref, o_ref):
    an = _rmsnorm(at_ref[...].astype(F32), ong_ref[...]).astype(BF16)
    mix = _dot(ys_ref[...], wo1_ref[...]) + _dot(an, wo2_ref[...])
    h1 = _layernorm(DEEPNORM_ALPHA * x_ref[...] + mix, g1_ref[...], b1_ref[...])
    hb = h1.astype(BF16)
    ffn = None
    for a, b in zip(_FF_SPLIT[:-1], _FF_SPLIT[1:]):
        gate = _dot(hb, wg_ref[:, a:b])
        up = _dot(hb, wu_ref[:, a:b])
        act = (gate * jax.nn.sigmoid(gate) * up).astype(BF16)
        part = _dot(act, wd_ref[a:b, :])
        ffn = part if ffn is None else ffn + part
    ple = jax.nn.sigmoid(_dot(hb, wpg_ref[...])) * _dot(p_ref[...].astype(BF16), wpp_ref[...])
    o_ref[...] = _layernorm(DEEPNORM_ALPHA * h1 + ffn + ple, g2_ref[...], b2_ref[...])


def _tail(ys, at, x2, p2, wo1, wo2, ong, g1, b1, wg, wu, wd, wpg, wpp, g2, b2, tm):
    n = x2.shape[0]
    row = lambda w: pl.BlockSpec((tm, w), lambda i: (i, 0))
    consts = (wo1, wo2, ong, g1, b1, wg, wu, wd, wpg, wpp, g2, b2)
    return pl.pallas_call(
        _tail_kernel,
        grid=(n // tm,),
        in_specs=[row(SSD_INNER), row(MLA_OUT), row(D_MODEL), row(PLE_DIM)] + [_const_spec(c.shape) for c in consts],
        out_specs=row(D_MODEL),
        out_shape=jax.ShapeDtypeStruct((n, D_MODEL), F32),
        compiler_params=pltpu.CompilerParams(dimension_semantics=("arbitrary",), vmem_limit_bytes=VMEM_LIMIT),
        name="tail",
    )(ys, at, x2, p2, *consts)


def _rope_tables(positions):
    half = MLA_ROPE // 2
    inv_freq = 1.0 / (ROPE_BASE ** (jnp.arange(0, MLA_ROPE, 2, dtype=F32) / MLA_ROPE))
    ang = positions.astype(F32).reshape(-1, 1) * inv_freq
    cos, sin = jnp.cos(ang), jnp.sin(ang)
    n = cos.shape[0]
    ones = jnp.ones((n, MLA_NOPE), F32)
    z = lambda w: jnp.zeros((n, w), F32)
    pad = HEAD_PAD - MLA_NOPE - MLA_ROPE
    cm = jnp.concatenate([ones, cos, cos, z(pad)], axis=1)
    s1 = jnp.concatenate([z(MLA_NOPE + half), sin, z(pad)], axis=1)
    s2 = jnp.concatenate([z(MLA_NOPE), -sin, z(half + pad)], axis=1)
    return cm, s1, s2


def kernel(x, p, positions, w_in, ssd_conv_w, ssd_conv_b, ssd_dt_bias, ssd_A_log, ssd_D, ssd_norm_w,
           mla_q_norm_w, mla_w_q_b, mla_kv_norm_w, mla_w_kv_b, mla_out_norm_w, w_out,
           ln_mix_g, ln_mix_b, w_ffn_gate, w_ffn_up, w_ffn_down, w_ple_gate, w_ple_proj,
           ln_ffn_g, ln_ffn_b):
    batch, seq, _ = x.shape
    n = batch * seq
    nchunk = seq // SSD_CHUNK
    x2 = x.reshape(n, D_MODEL)
    p2 = p[0].reshape(n, PLE_DIM)
    r2 = lambda a: a.reshape(1, -1)

    w = w_in[0]
    o = 0
    parts = {}
    for name, width in (("z", SSD_INNER), ("xbc", SSD_XBC), ("dt", SSD_HEADS), ("qc", MLA_Q_RANK),
                        ("kvc", MLA_KV_RANK), ("kr", MLA_ROPE)):
        parts[name] = w[:, o:o + width]
        o += width
    zc = lambda width: jnp.zeros((D_MODEL, width), w.dtype)
    w_small = jnp.concatenate([parts["dt"], zc(KR_OFF - SSD_HEADS), parts["kr"],
                               zc(SMALL_W - KR_OFF - MLA_ROPE)], axis=1)
    w_cat = jnp.concatenate([parts["z"], parts["xbc"], parts["qc"], parts["kvc"], w_small], axis=1).astype(BF16)

    qd = MLA_NOPE + MLA_ROPE
    wq = mla_w_q_b[0].reshape(MLA_Q_RANK, MLA_HEADS, qd)
    wq = jnp.pad(wq, ((0, 0), (0, 0), (0, HEAD_PAD - qd))).reshape(MLA_Q_RANK, MLA_HEADS * HEAD_PAD).astype(BF16)
    wkv = mla_w_kv_b[0].reshape(MLA_KV_RANK, MLA_HEADS, MLA_NOPE + MLA_V)
    wk = jnp.pad(wkv[:, :, :MLA_NOPE], ((0, 0), (0, 0), (0, HEAD_PAD - MLA_NOPE)))
    wk = wk.reshape(MLA_KV_RANK, MLA_HEADS * HEAD_PAD).astype(BF16)
    wv = wkv[:, :, MLA_NOPE:].reshape(MLA_KV_RANK, MLA_OUT).astype(BF16)

    cm, s1, s2 = _rope_tables(positions)

    z, xbc, small, q, k, v = _inproj(x2, w_cat, wq, wk, wv, r2(mla_q_norm_w[0]), r2(mla_kv_norm_w[0]),
                                     cm, s1, s2, tm=512)

    dtt = small[:, :SSD_HEADS].T
    dfull = jnp.repeat(ssd_D[0], SSD_HEAD_DIM).reshape(1, SSD_INNER)
    expand = jnp.repeat(jnp.eye(SSD_HEADS, dtype=BF16), SSD_HEAD_DIM, axis=1)
    y_ssd = _ssd(xbc, z, small, dtt, ssd_conv_w[0], r2(ssd_conv_b[0]), ssd_dt_bias[0], ssd_A_log[0],
                 dfull, r2(ssd_norm_w[0]), expand, batch, nchunk)

    attn = _attention(q, k, v, batch, seq, blk=512)

    wo = w_out[0].astype(BF16)
    out = _tail(y_ssd, attn, x2, p2, wo[:SSD_INNER], wo[SSD_INNER:], r2(mla_out_norm_w[0]),
                r2(ln_mix_g[0]), r2(ln_mix_b[0]),
                w_ffn_gate[0].astype(BF16), w_ffn_up[0].astype(BF16), w_ffn_down[0].astype(BF16),
                w_ple_gate[0].astype(BF16), w_ple_proj[0].astype(BF16),
                r2(ln_ffn_g[0]), r2(ln_ffn_b[0]), tm=512)
    return out.reshape(batch, seq, D_MODEL)
```
